```python
import math
import jax, jax.numpy as jnp
from jax import lax
import numpy as np

D_MODEL = 2048
BATCH = 8
SEQ = 2048
DEPTH = 2

GRID_W = 64
CTX_LEN = 256
N_EVEN = (DEPTH + 1) // 2
N_ODD = DEPTH // 2
EPS = 1e-6
N_MOD = 6
D_FF = 4 * D_MODEL

HEAD_DIM = 128
NA_HEADS = 8
NA_WIDTH = NA_HEADS * HEAD_DIM
NA_KR = 8
NA_KC = 16
LRU_WIDTH = D_MODEL - NA_WIDTH
LRU_BLOCKS = 8
LRU_BLOCK = LRU_WIDTH // LRU_BLOCKS
LRU_C = 8.0
CONV_W = 4
CONV_PAD_LO = 2
AB_IN = 3 * NA_WIDTH + 2 * LRU_WIDTH
MIX_WIDTH = NA_WIDTH + LRU_WIDTH

MLA_HEADS = 16
Q_LORA = 512
KV_LORA = 512
NOPE_DIM = 128
ROPE_DIM = 64
V_DIM = 128
ROPE_BASE = 10000.0
Q_BLOCK = 128
NEG_INF = -1e30

kernel_name = "hybrid_natten_rglru_mla_prefix_dit"


def rms_norm(x, gain):
    x32 = x.astype(jnp.float32)
    y = x32 * lax.rsqrt(jnp.mean(jnp.square(x32), axis=-1, keepdims=True) + EPS)
    return (y * gain.astype(jnp.float32)).astype(x.dtype)


def modulate(h, shift, scale):
    return h * (1.0 + scale) + shift


def split_heads(t, n_heads):
    b, n, _ = t.shape
    return t.reshape(b, n, n_heads, -1).transpose(0, 2, 1, 3)


def merge_heads(t):
    b, h, n, d = t.shape
    return t.transpose(0, 2, 1, 3).reshape(b, n, h * d)


def softmax_attend(q, k, v, scale):
    s = jnp.einsum("bhqd,bhkd->bhqk", q, k).astype(jnp.float32) * scale
    p = jax.nn.softmax(s, axis=-1).astype(v.dtype)
    return jnp.einsum("bhqk,bhkd->bhqd", p, v)


def blocked_attend(q, k, v, scale):
    b, h, n, d = q.shape
    nb = n // Q_BLOCK
    qb = q.reshape(b, h, nb, Q_BLOCK, d).transpose(2, 0, 1, 3, 4)
    ob = lax.map(lambda qi: softmax_attend(qi, k, v, scale), qb)
    return ob.transpose(1, 2, 0, 3, 4).reshape(b, h, n, -1)


def squared_relu_mlp(h, w1, w2):
    return jnp.square(jax.nn.relu(h @ w1)) @ w2


def rope_tables(n):
    pos = jnp.arange(n)
    row = (pos // GRID_W).astype(jnp.float32)
    col = (pos % GRID_W).astype(jnp.float32)
    axis_dim = ROPE_DIM // 2
    inv = jnp.power(ROPE_BASE, -jnp.arange(0, axis_dim, 2, dtype=jnp.float32) / axis_dim)
    ang = jnp.stack([row[:, None] * inv, col[:, None] * inv], axis=1)
    return jnp.cos(ang), jnp.sin(ang)


def apply_rope(t, cos, sin):
    t32 = t.astype(jnp.float32).reshape(t.shape[:-1] + (2, 2, ROPE_DIM // 4))
    x1, x2 = t32[..., 0, :], t32[..., 1, :]
    out = jnp.stack([x1 * cos - x2 * sin, x1 * sin + x2 * cos], axis=-2)
    return out.reshape(t.shape).astype(t.dtype)


def neighbourhood_attention(q, k, v, q_c, k_c, v_c, rpb):
    b, h, s, d = q.shape
    rows = s // GRID_W
    kr = min(NA_KR, rows)
    scale = d ** -0.5
    r = jnp.arange(rows)
    row_start = jnp.clip(r - kr // 2, 0, rows - kr)
    row_idx = row_start[:, None] + jnp.arange(kr)[None, :]
    col = jnp.arange(GRID_W)
    col_start = jnp.clip(col - NA_KC // 2, 0, GRID_W - NA_KC)
    col_ok = (col[None, :] >= col_start[:, None]) & (col[None, :] < col_start[:, None] + NA_KC)
    dr = row_idx - r[:, None] + (NA_KR - 1)
    dc = jnp.clip(col[None, :] - col[:, None], -(NA_KC - 1), NA_KC - 1) + (NA_KC - 1)
    bias = rpb[:, dr[:, None, :, None], dc[None, :, None, :]]

    qg = q.reshape(b, h, rows, GRID_W, d)
    kg = k.reshape(b, h, rows, GRID_W, d)
    vg = v.reshape(b, h, rows, GRID_W, d)
    k_win = jnp.take(kg, row_idx, axis=2)
    v_win = jnp.take(vg, row_idx, axis=2)
    s_win = (jnp.einsum("bhrqd,bhrkcd->bhrqkc", qg, k_win).astype(jnp.float32) * scale
             + bias[None].astype(jnp.float32))
    s_win = jnp.where(col_ok[:, None, :], s_win, NEG_INF)
    s_ctx = jnp.einsum("bhrqd,bhnd->bhrqn", qg, k_c).astype(jnp.float32) * scale
    n_win = kr * GRID_W
    s_all = jnp.concatenate([s_win.reshape(b, h, rows, GRID_W, n_win), s_ctx], axis=-1)
    p = jax.nn.softmax(s_all, axis=-1).astype(v.dtype)
    p_win = p[..., :n_win].reshape(b, h, rows, GRID_W, kr, GRID_W)
    p_ctx = p[..., n_win:]
    o = (jnp.einsum("bhrqkc,bhrkcd->bhrqd", p_win, v_win)
         + jnp.einsum("bhrqn,bhnd->bhrqd", p_ctx, v_c))
    o_lat = o.reshape(b, h, s, d)
    o_ctx = None if q_c is None else softmax_attend(q_c, k_c, v_c, scale)
    return o_lat, o_ctx


def depthwise_conv(x, w, bias):
    n = x.shape[1]
    xp = jnp.pad(x, ((0, 0), (CONV_PAD_LO, CONV_W - 1 - CONV_PAD_LO), (0, 0)))
    y = bias
    for tap in range(CONV_W):
        y = y + xp[:, tap:tap + n] * w[tap]
    return y


def rglru_coeffs(xc, wa, ba, wx, bx, lam):
    b, n, _ = xc.shape
    xb = xc.reshape(b, n, LRU_BLOCKS, LRU_BLOCK)
    gate_r = jax.nn.sigmoid(jnp.einsum("btni,nij->btnj", xb, wa).reshape(b, n, -1) + ba)
    gate_i = jax.nn.sigmoid(jnp.einsum("btni,nij->btnj", xb, wx).reshape(b, n, -1) + bx)
    log_a = -LRU_C * gate_r.astype(jnp.float32) * jax.nn.softplus(-lam.astype(jnp.float32))
    a = jnp.exp(log_a)
    u = jnp.sqrt(-jnp.expm1(2.0 * log_a)) * (gate_i * xc).astype(jnp.float32)
    return a, u


def linear_scan(a, u, reverse):
    def combine(left, right):
        a_l, u_l = left
        a_r, u_r = right
        return a_l * a_r, a_r * u_l + u_r
    _, hs = lax.associative_scan(combine, (a, u), reverse=reverse, axis=1)
    return hs


def bidirectional_rglru(u_lat, u_ctx, conv_w, conv_b, wa, ba, wx, bx, lam, want_ctx):
    xc_l = depthwise_conv(u_lat, conv_w, conv_b)
    xc_c = depthwise_conv(u_ctx, conv_w, conv_b)
    outs_l, outs_c = [], []
    for d, rev in ((0, False), (1, True)):
        a_c, v_c = rglru_coeffs(xc_c, wa[d], ba[d], wx[d], bx[d], lam[d])
        h_c = linear_scan(a_c, v_c, rev)
        end = 0 if rev else -1
        h0 = h_c[:, end]
        a_l, v_l = rglru_coeffs(xc_l, wa[d], ba[d], wx[d], bx[d], lam[d])
        start = -1 if rev else 0
        v_l = v_l.at[:, start].add(a_l[:, start] * h0)
        outs_l.append(linear_scan(a_l, v_l, rev))
        outs_c.append(h_c)
    y_lat = (outs_l[0] + outs_l[1]).astype(u_lat.dtype)
    y_ctx = (outs_c[0] + outs_c[1]).astype(u_ctx.dtype) if want_ctx else None
    return y_lat, y_ctx


def na_rglru_mixer(h_lat, h_ctx, w_in, w_out, rpb, conv_w, conv_b, wa, ba, wx, bx, lam, want_ctx):
    cuts = [NA_WIDTH, 2 * NA_WIDTH, 3 * NA_WIDTH, 3 * NA_WIDTH + LRU_WIDTH]
    q_l, k_l, v_l, u_l, g_l = jnp.split(h_lat @ w_in, cuts, axis=-1)
    w_q, w_k, w_v, w_u, w_g = jnp.split(w_in, cuts, axis=1)
    k_c, v_c, u_c = h_ctx @ w_k, h_ctx @ w_v, h_ctx @ w_u
    q_c = split_heads(h_ctx @ w_q, NA_HEADS) if want_ctx else None
    na_l, na_c = neighbourhood_attention(
        split_heads(q_l, NA_HEADS), split_heads(k_l, NA_HEADS), split_heads(v_l, NA_HEADS),
        q_c, split_heads(k_c, NA_HEADS), split_heads(v_c, NA_HEADS), rpb)
    r_l, r_c = bidirectional_rglru(u_l, u_c, conv_w, conv_b, wa, ba, wx, bx, lam, want_ctx)
    y_lat = jnp.concatenate([merge_heads(na_l), r_l * jax.nn.gelu(g_l)], axis=-1) @ w_out
    y_ctx = None
    if want_ctx:
        y_ctx = jnp.concatenate([merge_heads(na_c), r_c * jax.nn.gelu(h_ctx @ w_g)], axis=-1) @ w_out
    return y_lat, y_ctx


def mla_queries(h, w_dq, q_norm, w_uq, rope):
    q = split_heads(rms_norm(h @ w_dq, q_norm) @ w_uq, MLA_HEADS)
    if rope is not None:
        q = jnp.concatenate([q[..., :NOPE_DIM], apply_rope(q[..., NOPE_DIM:], *rope)], axis=-1)
    return q


def mla_keys_values(h, w_dkv, kv_norm, w_ukv, rope):
    p = h @ w_dkv
    c_kv, k_pe = p[..., :KV_LORA], p[..., KV_LORA:]
    kv = split_heads(rms_norm(c_kv, kv_norm) @ w_ukv, MLA_HEADS)
    k_nope, v = kv[..., :NOPE_DIM], kv[..., NOPE_DIM:]
    k_pe = k_pe[:, None]
    if rope is not None:
        k_pe = apply_rope(k_pe, *rope)
    k = jnp.concatenate([k_nope, jnp.broadcast_to(k_pe, k_nope.shape[:-1] + (ROPE_DIM,))], axis=-1)
    return k, v


def mla_mixer(h_lat, h_ctx, w_dq, w_dkv, q_norm, w_uq, kv_norm, w_ukv, w_o, want_ctx):
    scale = (NOPE_DIM + ROPE_DIM) ** -0.5
    rope = rope_tables(h_lat.shape[1])
    q_l = mla_queries(h_lat, w_dq, q_norm, w_uq, rope)
    k_l, v_l = mla_keys_values(h_lat, w_dkv, kv_norm, w_ukv, rope)
    k_c, v_c = mla_keys_values(h_ctx, w_dkv, kv_norm, w_ukv, None)
    k_all = jnp.concatenate([k_l, k_c], axis=2)
    v_all = jnp.concatenate([v_l, v_c], axis=2)
    y_lat = merge_heads(blocked_attend(q_l, k_all, v_all, scale)) @ w_o
    y_ctx = None
    if want_ctx:
        q_c = mla_queries(h_ctx, w_dq, q_norm, w_uq, None)
        y_ctx = merge_heads(softmax_attend(q_c, k_c, v_c, scale)) @ w_o
    return y_lat, y_ctx


def _dense(key, shape, fan_in, gain=1.0):
    return gain * jax.random.normal(key, shape, jnp.float32) * fan_in ** -0.5


def _normal(key, shape, std=1.0):
    return std * jax.random.normal(key, shape, jnp.float32)


def setup_inputs(seed: int = 0) -> dict:
    key = jax.random.key(seed)
    ks = jax.random.split(key, 28)
    D = D_MODEL
    a_base = jax.random.uniform(ks[19], (N_EVEN, 2, LRU_WIDTH), jnp.float32, 0.9, 0.999) ** (1.0 / LRU_C)
    return {
        "x": _normal(ks[0], (BATCH, SEQ, D)),
        "c": _normal(ks[1], (BATCH, D)),
        "ctx": _normal(ks[2], (BATCH, CTX_LEN, D)),
        "c_ctx": _normal(ks[3], (D,)),
        "mod_w": _dense(ks[4], (DEPTH, D, N_MOD * D), D, 0.5),
        "mod_b": _normal(ks[5], (DEPTH, N_MOD * D), 0.02),
        "norm_mix": 1.0 + _normal(ks[6], (DEPTH, D), 0.02),
        "norm_mlp": 1.0 + _normal(ks[7], (DEPTH, D), 0.02),
        "mlp_w1": _dense(ks[8], (DEPTH, D, D_FF), D),
        "mlp_w2": _dense(ks[9], (DEPTH, D_FF, D), D_FF),
        "ab_w_in": _dense(ks[10], (N_EVEN, D, AB_IN), D),
        "ab_w_out": _dense(ks[11], (N_EVEN, MIX_WIDTH, D), MIX_WIDTH),
        "na_rpb": _normal(ks[12], (N_EVEN, NA_HEADS, 2 * NA_KR - 1, 2 * NA_KC - 1), 0.1),
        "lru_conv_w": _dense(ks[13], (N_EVEN, CONV_W, LRU_WIDTH), CONV_W),
        "lru_conv_b": _normal(ks[14], (N_EVEN, LRU_WIDTH), 0.02),
        "lru_wa": _dense(ks[15], (N_EVEN, 2, LRU_BLOCKS, LRU_BLOCK, LRU_BLOCK), LRU_BLOCK),
        "lru_ba": _normal(ks[16], (N_EVEN, 2, LRU_WIDTH), 0.02),
        "lru_wx": _dense(ks[17], (N_EVEN, 2, LRU_BLOCKS, LRU_BLOCK, LRU_BLOCK), LRU_BLOCK),
        "lru_bx": _normal(ks[18], (N_EVEN, 2, LRU_WIDTH), 0.02),
        "lru_lambda": jnp.log(a_base) - jnp.log1p(-a_base),
        "mla_w_dq": _dense(ks[20], (N_ODD, D, Q_LORA), D),
        "mla_w_dkv": _dense(ks[21], (N_ODD, D, KV_LORA + ROPE_DIM), D),
        "mla_q_norm": 1.0 + _normal(ks[22], (N_ODD, Q_LORA), 0.02),
        "mla_w_uq": _dense(ks[23], (N_ODD, Q_LORA, MLA_HEADS * (NOPE_DIM + ROPE_DIM)), Q_LORA),
        "mla_kv_norm": 1.0 + _normal(ks[24], (N_ODD, KV_LORA), 0.02),
        "mla_w_ukv": _dense(ks[25], (N_ODD, KV_LORA, MLA_HEADS * (NOPE_DIM + V_DIM)), KV_LORA),
        "mla_w_o": _dense(ks[26], (N_ODD, MLA_HEADS * V_DIM, D), MLA_HEADS * V_DIM),
        "final_norm": 1.0 + _normal(ks[27], (D,), 0.02),
    }


def reference(x, c, ctx, c_ctx, mod_w, mod_b, norm_mix, norm_mlp, mlp_w1, mlp_w2,
              ab_w_in, ab_w_out, na_rpb, lru_conv_w, lru_conv_b, lru_wa, lru_ba, lru_wx, lru_bx,
              lru_lambda, mla_w_dq, mla_w_dkv, mla_q_norm, mla_w_uq, mla_kv_norm, mla_w_ukv,
              mla_w_o, final_norm):
    z = ctx
    cond_lat = jax.nn.silu(c)
    cond_ctx = jax.nn.silu(c_ctx)
    for layer in range(DEPTH):
        last = layer == DEPTH - 1
        want_ctx = not last
        j = layer // 2
        m_l = jnp.split((cond_lat @ mod_w[layer] + mod_b[layer])[:, None, :], N_MOD, axis=-1)
        m_c = jnp.split(cond_ctx @ mod_w[layer] + mod_b[layer], N_MOD, axis=-1)
        h_lat = modulate(rms_norm(x, norm_mix[layer]), m_l[0], m_l[1])
        h_ctx = modulate(rms_norm(z, norm_mix[layer]), m_c[0], m_c[1])
        if layer % 2 == 0:
            y_lat, y_ctx = na_rglru_mixer(
                h_lat, h_ctx, ab_w_in[j], ab_w_out[j], na_rpb[j], lru_conv_w[j], lru_conv_b[j],
                lru_wa[j], lru_ba[j], lru_wx[j], lru_bx[j], lru_lambda[j], want_ctx)
        else:
            y_lat, y_ctx = mla_mixer(
                h_lat, h_ctx, mla_w_dq[j], mla_w_dkv[j], mla_q_norm[j], mla_w_uq[j],
                mla_kv_norm[j], mla_w_ukv[j], mla_w_o[j], want_ctx)
        x = x + m_l[2] * y_lat
        x = x + m_l[5] * squared_relu_mlp(
            modulate(rms_norm(x, norm_mlp[layer]), m_l[3], m_l[4]), mlp_w1[layer], mlp_w2[layer])
        if want_ctx:
            z = z + m_c[2] * y_ctx
            z = z + m_c[5] * squared_relu_mlp(
                modulate(rms_norm(z, norm_mlp[layer]), m_c[3], m_c[4]), mlp_w1[layer], mlp_w2[layer])
    return rms_norm(x, final_norm)
```

```python
import functools

import numpy as np
import jax
import jax.numpy as jnp
from jax import lax
from jax.experimental import pallas as pl
from jax.experimental.pallas import tpu as pltpu

F32 = jnp.float32
BF16 = jnp.bfloat16

D = 2048
B = 8
S = 2048
C = 256
GRID_W = 64
ROWS = S // GRID_W
EPS = 1e-6
N_MOD = 6
D_FF = 4 * D
HD = 128
NA_HEADS = 8
NA_W = NA_HEADS * HD
NA_KR = 8
NA_KC = 16
LRU_W = D - NA_W
LRU_BLOCKS = 8
LRU_C = 8.0
AB_IN = 3 * NA_W + 2 * LRU_W
MLA_HEADS = 16
Q_LORA = 512
KV_LORA = 512
NOPE = 128
ROPE = 64
ROPE_BASE = 10000.0
NEG_INF = -1e30

MIB = 1 << 20
VMEM_CAP = 58 * MIB

NA_QR = 4
NA_KROWS = NA_QR + NA_KR - 1
NA_TQ = NA_QR * GRID_W
NA_TK = NA_KROWS * GRID_W
NA_BLOCKS = ROWS // NA_QR

LRU_TC = 32
LRU_T = C + S
LRU_CTX0 = 2
LRU_LAT0 = LRU_CTX0 + C + 2
LRU_UROWS = LRU_LAT0 + S + 4


def _cparams(sem, vmem_bytes):
    want = vmem_bytes + vmem_bytes // 4 + 8 * MIB
    return pltpu.CompilerParams(dimension_semantics=sem,
                                vmem_limit_bytes=int(min(max(want, 24 * MIB), VMEM_CAP)))


def _sigmoid(x):
    return 0.5 * jnp.tanh(0.5 * x) + 0.5


def _norm_mod(x, gain, shift, scale):
    ms = jnp.mean(x * x, axis=-1, keepdims=True)
    y = x * lax.rsqrt(ms + EPS) * gain
    return y * (1.0 + scale) + shift


def _rms(x, gain):
    ms = jnp.mean(x * x, axis=-1, keepdims=True)
    return x * lax.rsqrt(ms + EPS) * gain


def _mod_kernel(cond_ref, w_ref, b_ref, o_ref):
    c = cond_ref[...]
    s = c * _sigmoid(c)
    o_ref[...] = jnp.dot(s.astype(BF16), w_ref[...].astype(BF16),
                         preferred_element_type=F32) + b_ref[...]


def _mod_call(cond, mod_w, mod_b):
    depth, _, n = mod_w.shape
    g = cond.shape[0]
    tn = 1024
    return pl.pallas_call(
        _mod_kernel,
        grid=(depth, n // tn),
        in_specs=[pl.BlockSpec((g, D), lambda l, j: (0, 0)),
                  pl.BlockSpec((None, D, tn), lambda l, j: (l, 0, j)),
                  pl.BlockSpec((None, 1, tn), lambda l, j: (l, 0, j))],
        out_specs=pl.BlockSpec((None, g, tn), lambda l, j: (l, 0, j)),
        out_shape=jax.ShapeDtypeStruct((depth, g, n), F32),
        compiler_params=_cparams(("arbitrary", "arbitrary"), 2 * D * tn * 4 + 8 * MIB),
        name="adaln_mod",
    )(cond, mod_w, mod_b.reshape(depth, 1, n))


def _proj_kernel(x_ref, gain_ref, shift_ref, scale_ref, w_ref, o_ref, h_ref):
    @pl.when(pl.program_id(1) == 0)
    def _():
        h_ref[...] = _norm_mod(x_ref[...], gain_ref[...], shift_ref[...], scale_ref[...]).astype(BF16)

    o_ref[...] = jnp.dot(h_ref[...], w_ref[...], preferred_element_type=F32).astype(o_ref.dtype)


def _proj_call(x, gain, shift, scale, w, rows_per_group, tm, tn):
    m = x.shape[0]
    n = w.shape[1]
    tpg = rows_per_group // tm
    vmem = 2 * tm * D * 4 + tm * D * 2 + 2 * D * tn * 2 + 2 * tm * tn * 2 + tm * tn * 4 + 6 * MIB
    return pl.pallas_call(
        _proj_kernel,
        grid=(m // tm, n // tn),
        in_specs=[pl.BlockSpec((tm, D), lambda i, j: (i, 0)),
                  pl.BlockSpec((1, D), lambda i, j: (0, 0)),
                  pl.BlockSpec((None, 1, D), lambda i, j: (i // tpg, 0, 0)),
                  pl.BlockSpec((None, 1, D), lambda i, j: (i // tpg, 0, 0)),
                  pl.BlockSpec((D, tn), lambda i, j: (0, j))],
        out_specs=pl.BlockSpec((tm, tn), lambda i, j: (i, j)),
        out_shape=jax.ShapeDtypeStruct((m, n), BF16),
        scratch_shapes=[pltpu.VMEM((tm, D), BF16)],
        compiler_params=_cparams(("arbitrary", "arbitrary"), vmem),
        name="norm_mod_proj",
    )(x, gain, shift, scale, w)


def _mm_kernel(a_ref, w_ref, o_ref):
    o_ref[...] = jnp.dot(a_ref[...], w_ref[...], preferred_element_type=F32).astype(o_ref.dtype)


def _mm_call(a, w, tm, tn):
    m, k = a.shape
    n = w.shape[1]
    vmem = 2 * tm * k * 2 + 2 * k * tn * 2 + 2 * tm * tn * 2 + tm * tn * 4 + 6 * MIB
    return pl.pallas_call(
        _mm_kernel,
        grid=(m // tm, n // tn),
        in_specs=[pl.BlockSpec((tm, k), lambda i, j: (i, 0)),
                  pl.BlockSpec((k, tn), lambda i, j: (0, j))],
        out_specs=pl.BlockSpec((tm, tn), lambda i, j: (i, j)),
        out_shape=jax.ShapeDtypeStruct((m, n), BF16),
        compiler_params=_cparams(("arbitrary", "arbitrary"), vmem),
        name="matmul",
    )(a, w)


def _out_kernel(*refs, n_a):
    a_refs = refs[:n_a]
    w_ref, x_ref, gate_ref, o_ref = refs[n_a:]
    y = None
    k0 = 0
    for a_ref in a_refs:
        k = a_ref.shape[1]
        part = jnp.dot(a_ref[...], w_ref[k0:k0 + k, :], preferred_element_type=F32)
        y = part if y is None else y + part
        k0 += k
    o_ref[...] = x_ref[...] + gate_ref[...] * y


def _out_call(a_list, w, x, gate, rows_per_group, tm):
    m = x.shape[0]
    tpg = rows_per_group // tm
    ktot = w.shape[0]
    vmem = 2 * tm * ktot * 2 + 2 * ktot * D * 2 + 4 * tm * D * 4 + tm * D * 4 + 6 * MIB
    return pl.pallas_call(
        functools.partial(_out_kernel, n_a=len(a_list)),
        grid=(m // tm,),
        in_specs=[pl.BlockSpec((tm, a.shape[1]), lambda i: (i, 0)) for a in a_list]
        + [pl.BlockSpec((ktot, D), lambda i: (0, 0)),
           pl.BlockSpec((tm, D), lambda i: (i, 0)),
           pl.BlockSpec((None, 1, D), lambda i: (i // tpg, 0, 0))],
        out_specs=pl.BlockSpec((tm, D), lambda i: (i, 0)),
        out_shape=jax.ShapeDtypeStruct((m, D), F32),
        compiler_params=_cparams(("arbitrary",), vmem),
        name="out_proj_residual",
    )(*a_list, w, x, gate)


def _mlp_kernel(x_ref, gain_ref, shift_ref, scale_ref, gate_ref, w1_ref, w2_ref, fgain_ref,
                o_ref, h_ref, acc_ref, *, final_norm):
    f = pl.program_id(1)

    @pl.when(f == 0)
    def _():
        h_ref[...] = _norm_mod(x_ref[...], gain_ref[...], shift_ref[...], scale_ref[...]).astype(BF16)
        acc_ref[...] = jnp.zeros_like(acc_ref)

    a = jnp.dot(h_ref[...], w1_ref[...], preferred_element_type=F32)
    a = jnp.square(jnp.maximum(a, 0.0)).astype(BF16)
    acc_ref[...] += jnp.dot(a, w2_ref[...], preferred_element_type=F32)

    @pl.when(f == pl.num_programs(1) - 1)
    def _():
        y = x_ref[...] + gate_ref[...] * acc_ref[...]
        if final_norm:
            y = _rms(y, fgain_ref[...])
        o_ref[...] = y


def _mlp_call(x, gain, shift, scale, gate, w1, w2, rows_per_group, tm, tf, final_gain=None):
    m = x.shape[0]
    tpg = rows_per_group // tm
    final_norm = final_gain is not None
    fg = final_gain if final_norm else gain
    vmem = (4 * tm * D * 4 + tm * D * 2 + tm * D * 4 + 4 * D * tf * 2 + tm * tf * 6 + 6 * MIB)
    modspec = pl.BlockSpec((None, 1, D), lambda i, f: (i // tpg, 0, 0))
    vecspec = pl.BlockSpec((1, D), lambda i, f: (0, 0))
    return pl.pallas_call(
        functools.partial(_mlp_kernel, final_norm=final_norm),
        grid=(m // tm, D_FF // tf),
        in_specs=[pl.BlockSpec((tm, D), lambda i, f: (i, 0)),
                  vecspec, modspec, modspec, modspec,
                  pl.BlockSpec((D, tf), lambda i, f: (0, f)),
                  pl.BlockSpec((tf, D), lambda i, f: (f, 0)),
                  vecspec],
        out_specs=pl.BlockSpec((tm, D), lambda i, f: (i, 0)),
        out_shape=jax.ShapeDtypeStruct((m, D), F32),
        scratch_shapes=[pltpu.VMEM((tm, D), BF16), pltpu.VMEM((tm, D), F32)],
        compiler_params=_cparams(("arbitrary", "arbitrary"), vmem),
        name="mlp_relu2",
    )(x, gain, shift, scale, gate, w1, w2, fg)


def _na_block_start(blk):
    return min(max(NA_QR * blk - NA_KR // 2, 0), ROWS - NA_KROWS)


def _na_bias_table(rpb):
    cfgs = []
    for blk in (0, 1, NA_BLOCKS - 1):
        start = _na_block_start(blk)
        r = NA_QR * blk + np.arange(NA_QR)
        kr = start + np.arange(NA_KROWS)
        row_start = np.clip(r - NA_KR // 2, 0, ROWS - NA_KR)
        row_ok = (kr[None, :] >= row_start[:, None]) & (kr[None, :] < row_start[:, None] + NA_KR)
        dr = np.clip(kr[None, :] - r[:, None] + (NA_KR - 1), 0, 2 * NA_KR - 2)
        col = np.arange(GRID_W)
        col_start = np.clip(col - NA_KC // 2, 0, GRID_W - NA_KC)
        col_ok = (col[None, :] >= col_start[:, None]) & (col[None, :] < col_start[:, None] + NA_KC)
        dc = np.clip(col[None, :] - col[:, None], -(NA_KC - 1), NA_KC - 1) + (NA_KC - 1)
        shape = (NA_QR, GRID_W, NA_KROWS, GRID_W)
        dr_full = np.broadcast_to(dr[:, None, :, None], shape).reshape(NA_TQ, NA_TK)
        dc_full = np.broadcast_to(dc[None, :, None, :], shape).reshape(NA_TQ, NA_TK)
        ok = np.broadcast_to(row_ok[:, None, :, None] & col_ok[None, :, None, :], shape).reshape(NA_TQ, NA_TK)
        bias = rpb[:, dr_full, dc_full]
        cfgs.append(jnp.where(ok[None], bias, NEG_INF))
    return jnp.stack(cfgs, axis=1)


def _softmax_pv(scores, values):
    m = None
    for s in scores:
        mk = jnp.max(s, axis=-1, keepdims=True)
        m = mk if m is None else jnp.maximum(m, mk)
    denom = None
    out = None
    for s, v in zip(scores, values):
        p = jnp.exp(s - m)
        dk = jnp.sum(p, axis=-1, keepdims=True)
        ok = jnp.dot(p.astype(BF16), v, preferred_element_type=F32)
        denom = dk if denom is None else denom + dk
        out = ok if out is None else out + ok
    return out / denom


def _qk(q, k):
    return lax.dot_general(q, k, (((1,), (1,)), ((), ())), preferred_element_type=F32)


def _na_kernel(q_ref, k_ref, v_ref, qc_ref, kc_ref, vc_ref, bias_ref, o_ref, oc_ref):
    scale = HD ** -0.5
    kc = kc_ref[...]
    vc = vc_ref[...]
    for blk in range(NA_BLOCKS):
        cfg = 0 if blk == 0 else (2 if blk == NA_BLOCKS - 1 else 1)
        k0 = _na_block_start(blk) * GRID_W
        q = q_ref[blk * NA_TQ:(blk + 1) * NA_TQ, :]
        s_win = _qk(q, k_ref[k0:k0 + NA_TK, :]) * scale + bias_ref[cfg]
        s_ctx = _qk(q, kc) * scale
        o = _softmax_pv([s_win, s_ctx], [v_ref[k0:k0 + NA_TK, :], vc])
        o_ref[blk * NA_TQ:(blk + 1) * NA_TQ, :] = o.astype(o_ref.dtype)
    s_cc = _qk(qc_ref[...], kc) * scale
    oc_ref[...] = _softmax_pv([s_cc], [vc]).astype(oc_ref.dtype)


def _na_call(p_lat, p_ctx, bias):
    lat = lambda off: pl.BlockSpec((S, HD), lambda h, b: (b, off + h))
    ctx = lambda off: pl.BlockSpec((C, HD), lambda h, b: (b, off + h))
    return pl.pallas_call(
        _na_kernel,
        grid=(NA_HEADS, B),
        in_specs=[lat(0), lat(NA_HEADS), lat(2 * NA_HEADS),
                  ctx(0), ctx(NA_HEADS), ctx(2 * NA_HEADS),
                  pl.BlockSpec((None, 3, NA_TQ, NA_TK), lambda h, b: (h, 0, 0, 0))],
        out_specs=[pl.BlockSpec((S, HD), lambda h, b: (b, h)),
                   pl.BlockSpec((C, HD), lambda h, b: (b, h))],
        out_shape=[jax.ShapeDtypeStruct((B * S, NA_W), BF16),
                   jax.ShapeDtypeStruct((B * C, NA_W), BF16)],
        compiler_params=_cparams(("arbitrary", "arbitrary"), 32 * MIB),
        name="neighbourhood_attention",
    )(p_lat, p_lat, p_lat, p_ctx, p_ctx, p_ctx, bias)


def _gelu_tanh(x):
    return 0.5 * x * (1.0 + jnp.tanh(np.sqrt(2.0 / np.pi) * (x + 0.044715 * (x * x * x))))


def _lru_kernel(ul_ref, uc_ref, gl_ref, gc_ref, cw_ref, cb_ref, wa_ref, ba_ref, wx_ref, bx_ref,
                lam_ref, ol_ref, oc_ref, ut_ref, yt_ref):
    tc = LRU_TC
    chunk = 256
    ut_ref[0:LRU_CTX0] = jnp.zeros((LRU_CTX0, B, HD), F32)
    ut_ref[LRU_CTX0:LRU_CTX0 + C] = jnp.swapaxes(uc_ref[...].astype(F32), 0, 1)
    ut_ref[LRU_CTX0 + C:LRU_LAT0] = jnp.zeros((LRU_LAT0 - LRU_CTX0 - C, B, HD), F32)
    for c in range(S // chunk):
        ut_ref[LRU_LAT0 + c * chunk:LRU_LAT0 + (c + 1) * chunk] = jnp.swapaxes(
            ul_ref[:, c * chunk:(c + 1) * chunk, :].astype(F32), 0, 1)
    ut_ref[LRU_LAT0 + S:LRU_UROWS] = jnp.zeros((LRU_UROWS - LRU_LAT0 - S, B, HD), F32)

    cw = cw_ref[...]
    cb = cb_ref[...]
    lam = lam_ref[...]
    neg = -lam
    softplus = jnp.maximum(neg, 0.0) + jnp.log1p(jnp.exp(-jnp.abs(neg)))

    def chunk_step(base_u, base_y, d, h, reverse, accumulate):
        xc = cb
        for tap in range(4):
            xc = xc + ut_ref[pl.ds(base_u + (tap - 2), tc)] * cw[tap:tap + 1, :]
        x2 = xc.reshape(tc * B, HD)
        xb = x2.astype(BF16)
        gate_r = _sigmoid(jnp.dot(xb, wa_ref[d], preferred_element_type=F32) + ba_ref[d:d + 1, :])
        gate_i = _sigmoid(jnp.dot(xb, wx_ref[d], preferred_element_type=F32) + bx_ref[d:d + 1, :])
        log_a = (-LRU_C * softplus[d:d + 1, :]) * gate_r
        a = jnp.exp(log_a)
        one_minus_a2 = -jnp.tanh(log_a) * (a * a + 1.0)
        v = jnp.sqrt(one_minus_a2) * (gate_i * x2)
        a3 = a.reshape(tc, B, HD)
        v3 = v.reshape(tc, B, HD)
        hs = [None] * tc
        for i in (range(tc - 1, -1, -1) if reverse else range(tc)):
            h = a3[i] * h + v3[i]
            hs[i] = h
        y = jnp.stack(hs, axis=0)
        if accumulate:
            yt_ref[pl.ds(base_y, tc)] += y
        else:
            yt_ref[pl.ds(base_y, tc)] = y
        return h

    h0 = jnp.zeros((B, HD), F32)
    n_ctx = C // tc
    n_lat = S // tc
    h = lax.fori_loop(0, n_ctx, lambda c, h: chunk_step(LRU_CTX0 + c * tc, c * tc, 0, h, False, False), h0)
    lax.fori_loop(0, n_lat, lambda c, h: chunk_step(LRU_LAT0 + c * tc, C + c * tc, 0, h, False, False), h)
    h = lax.fori_loop(0, n_ctx, lambda c, h: chunk_step(
        LRU_CTX0 + (n_ctx - 1 - c) * tc, (n_ctx - 1 - c) * tc, 1, h, True, True), h0)
    lax.fori_loop(0, n_lat, lambda c, h: chunk_step(
        LRU_LAT0 + (n_lat - 1 - c) * tc, C + (n_lat - 1 - c) * tc, 1, h, True, True), h)

    yc = jnp.swapaxes(yt_ref[0:C], 0, 1)
    oc_ref[...] = (yc * _gelu_tanh(gc_ref[...].astype(F32))).astype(oc_ref.dtype)
    for c in range(S // chunk):
        yl = jnp.swapaxes(yt_ref[C + c * chunk:C + (c + 1) * chunk], 0, 1)
        g = gl_ref[:, c * chunk:(c + 1) * chunk, :].astype(F32)
        ol_ref[:, c * chunk:(c + 1) * chunk, :] = (yl * _gelu_tanh(g)).astype(ol_ref.dtype)


def _lru_call(p_lat, p_ctx, conv_w, conv_b, wa, ba, wx, bx, lam):
    u_off = 3 * NA_W // HD
    g_off = u_off + LRU_W // HD
    p_lat3 = p_lat.reshape(B, S, AB_IN)
    p_ctx3 = p_ctx.reshape(B, C, AB_IN)
    lat = lambda off: pl.BlockSpec((B, S, HD), lambda n: (0, 0, off + n))
    ctx = lambda off: pl.BlockSpec((B, C, HD), lambda n: (0, 0, off + n))
    vec = lambda rows: pl.BlockSpec((rows, HD), lambda n: (0, n))
    wspec = pl.BlockSpec((2, None, HD, HD), lambda n: (0, n, 0, 0))
    vmem = (LRU_UROWS + LRU_T) * B * HD * 4 + 2 * 3 * (B * S * HD * 2) + 2 * 3 * (B * C * HD * 2) + 8 * MIB
    o_lat, o_ctx = pl.pallas_call(
        _lru_kernel,
        grid=(LRU_BLOCKS,),
        in_specs=[lat(u_off), ctx(u_off), lat(g_off), ctx(g_off),
                  vec(4), vec(1), wspec, vec(2), wspec, vec(2), vec(2)],
        out_specs=[pl.BlockSpec((B, S, HD), lambda n: (0, 0, n)),
                   pl.BlockSpec((B, C, HD), lambda n: (0, 0, n))],
        out_shape=[jax.ShapeDtypeStruct((B, S, LRU_W), BF16),
                   jax.ShapeDtypeStruct((B, C, LRU_W), BF16)],
        scratch_shapes=[pltpu.VMEM((LRU_UROWS, B, HD), F32), pltpu.VMEM((LRU_T, B, HD), F32)],
        compiler_params=_cparams(("arbitrary",), vmem),
        name="bidirectional_rglru",
    )(p_lat3, p_ctx3, p_lat3, p_ctx3, conv_w, conv_b.reshape(1, LRU_W), wa, ba, wx, bx, lam)
    return o_lat.reshape(B * S, LRU_W), o_ctx.reshape(B * C, LRU_W)


def _rope_tables():
    pos = np.arange(S)
    row = (pos // GRID_W).astype(np.float32)
    col = (pos % GRID_W).astype(np.float32)
    axis_dim = ROPE // 2
    inv = jnp.power(ROPE_BASE, -jnp.arange(0, axis_dim, 2, dtype=F32) / axis_dim)
    ang_r = jnp.asarray(row)[:, None] * inv
    ang_c = jnp.asarray(col)[:, None] * inv
    cos = jnp.concatenate([jnp.cos(ang_r), jnp.cos(ang_r), jnp.cos(ang_c), jnp.cos(ang_c)], axis=-1)
    sin = jnp.concatenate([-jnp.sin(ang_r), jnp.sin(ang_r), -jnp.sin(ang_c), jnp.sin(ang_c)], axis=-1)
    return jnp.concatenate([cos, sin], axis=-1)


def _swap_cols(w):
    q = ROPE // 4
    return jnp.concatenate([w[..., q:2 * q], w[..., :q], w[..., 3 * q:], w[..., 2 * q:3 * q]], axis=-1)


def _rope_from_pair(pair, cs):
    r = pair * cs
    r = r + pltpu.roll(r, ROPE, axis=1)
    lane = lax.broadcasted_iota(jnp.int32, r.shape, 1)
    return jnp.where(lane < ROPE, r, 0.0)


def _mla_down_kernel(x_ref, gain_ref, shift_ref, scale_ref, w_ref, qn_ref, kvn_ref, cs_ref,
                     *out_refs, has_q, use_rope):
    h = _norm_mod(x_ref[...], gain_ref[...], shift_ref[...], scale_ref[...]).astype(BF16)
    p = jnp.dot(h, w_ref[...], preferred_element_type=F32)
    off = 0
    outs = list(out_refs)
    if has_q:
        oq_ref = outs.pop(0)
        oq_ref[...] = _rms(p[:, :Q_LORA], qn_ref[...]).astype(oq_ref.dtype)
        off = Q_LORA
    okv_ref, okp_ref = outs
    okv_ref[...] = _rms(p[:, off:off + KV_LORA], kvn_ref[...]).astype(okv_ref.dtype)
    pair = p[:, off + KV_LORA:off + KV_LORA + 2 * ROPE]
    if use_rope:
        kp = _rope_from_pair(pair, cs_ref[...])
    else:
        lane = lax.broadcasted_iota(jnp.int32, pair.shape, 1)
        kp = jnp.where(lane < ROPE, pair, 0.0)
    okp_ref[...] = kp.astype(okp_ref.dtype)


def _mla_down_call(x, gain, shift, scale, w, q_norm, kv_norm, cs, rows_per_group, tm, has_q, use_rope):
    m = x.shape[0]
    n = w.shape[1]
    tpg = rows_per_group // tm
    tps = S // tm if use_rope else 1
    out_shape = [jax.ShapeDtypeStruct((m, KV_LORA), BF16), jax.ShapeDtypeStruct((m, 2 * ROPE), BF16)]
    out_specs = [pl.BlockSpec((tm, KV_LORA), lambda i: (i, 0)), pl.BlockSpec((tm, 2 * ROPE), lambda i: (i, 0))]
    if has_q:
        out_shape.insert(0, jax.ShapeDtypeStruct((m, Q_LORA), BF16))
        out_specs.insert(0, pl.BlockSpec((tm, Q_LORA), lambda i: (i, 0)))
    vmem = 2 * tm * D * 4 + tm * D * 2 + 2 * D * n * 2 + tm * n * 4 + 4 * tm * n * 2 + 6 * MIB
    return pl.pallas_call(
        functools.partial(_mla_down_kernel, has_q=has_q, use_rope=use_rope),
        grid=(m // tm,),
        in_specs=[pl.BlockSpec((tm, D), lambda i: (i, 0)),
                  pl.BlockSpec((1, D), lambda i: (0, 0)),
                  pl.BlockSpec((None, 1, D), lambda i: (i // tpg, 0, 0)),
                  pl.BlockSpec((None, 1, D), lambda i: (i // tpg, 0, 0)),
                  pl.BlockSpec((D, n), lambda i: (0, 0)),
                  pl.BlockSpec((1, Q_LORA), lambda i: (0, 0)),
                  pl.BlockSpec((1, KV_LORA), lambda i: (0, 0)),
                  pl.BlockSpec((tm, 2 * ROPE), lambda i: (i % tps, 0))],
        out_specs=out_specs,
        out_shape=out_shape,
        compiler_params=_cparams(("arbitrary",), vmem),
        name="mla_down_proj",
    )(x, gain, shift, scale, w, q_norm, kv_norm, cs)


def _uq_kernel(a_ref, w_ref, cs_ref, qn_ref, qp_ref, *, heads):
    p = jnp.dot(a_ref[...], w_ref[...], preferred_element_type=F32)
    cs = cs_ref[...]
    width = NOPE + 2 * ROPE
    for hh in range(heads):
        qn_ref[:, hh * NOPE:(hh + 1) * NOPE] = p[:, hh * width:hh * width + NOPE].astype(qn_ref.dtype)
        pair = p[:, hh * width + NOPE:(hh + 1) * width]
        qp_ref[:, hh * 2 * ROPE:(hh + 1) * 2 * ROPE] = _rope_from_pair(pair, cs).astype(qp_ref.dtype)


def _uq_call(a, w, cs, tm, heads_per_step):
    m, k = a.shape
    width = NOPE + 2 * ROPE
    tn = heads_per_step * width
    tps = S // tm
    vmem = 2 * tm * k * 2 + 2 * k * tn * 2 + tm * tn * 4 + 4 * tm * heads_per_step * HD * 2 + 8 * MIB
    return pl.pallas_call(
        functools.partial(_uq_kernel, heads=heads_per_step),
        grid=(m // tm, MLA_HEADS // heads_per_step),
        in_specs=[pl.BlockSpec((tm, k), lambda i, j: (i, 0)),
                  pl.BlockSpec((k, tn), lambda i, j: (0, j)),
                  pl.BlockSpec((tm, 2 * ROPE), lambda i, j: (i % tps, 0))],
        out_specs=[pl.BlockSpec((tm, heads_per_step * NOPE), lambda i, j: (i, j)),
                   pl.BlockSpec((tm, heads_per_step * 2 * ROPE), lambda i, j: (i, j))],
        out_shape=[jax.ShapeDtypeStruct((m, MLA_HEADS * NOPE), BF16),
                   jax.ShapeDtypeStruct((m, MLA_HEADS * 2 * ROPE), BF16)],
        compiler_params=_cparams(("arbitrary", "arbitrary"), vmem),
        name="mla_q_up_rope",
    )(a, w, cs)


MLA_TQ = 256


def _mla_attn_kernel(qn_ref, qp_ref, knl_ref, vl_ref, kpl_ref, knc_ref, vc_ref, kpc_ref, o_ref):
    scale = (NOPE + ROPE) ** -0.5
    kl = jnp.concatenate([knl_ref[...], kpl_ref[...]], axis=-1)
    kc = jnp.concatenate([knc_ref[...], kpc_ref[...]], axis=-1)
    vl = vl_ref[...]
    vc = vc_ref[...]
    for qi in range(S // MLA_TQ):
        rows = slice(qi * MLA_TQ, (qi + 1) * MLA_TQ)
        q = jnp.concatenate([qn_ref[rows, :], qp_ref[rows, :]], axis=-1)
        s_lat = _qk(q, kl) * scale
        s_ctx = _qk(q, kc) * scale
        o_ref[rows, :] = _softmax_pv([s_lat, s_ctx], [vl, vc]).astype(o_ref.dtype)


def _mla_attn_call(qn, qp, kv_lat, kp_lat, kv_ctx, kp_ctx):
    return pl.pallas_call(
        _mla_attn_kernel,
        grid=(B, MLA_HEADS),
        in_specs=[pl.BlockSpec((S, HD), lambda b, h: (b, h)),
                  pl.BlockSpec((S, HD), lambda b, h: (b, h)),
                  pl.BlockSpec((S, HD), lambda b, h: (b, 2 * h)),
                  pl.BlockSpec((S, HD), lambda b, h: (b, 2 * h + 1)),
                  pl.BlockSpec((S, HD), lambda b, h: (b, 0)),
                  pl.BlockSpec((C, HD), lambda b, h: (b, 2 * h)),
                  pl.BlockSpec((C, HD), lambda b, h: (b, 2 * h + 1)),
                  pl.BlockSpec((C, HD), lambda b, h: (b, 0))],
        out_specs=pl.BlockSpec((S, HD), lambda b, h: (b, h)),
        out_shape=jax.ShapeDtypeStruct((B * S, MLA_HEADS * HD), BF16),
        compiler_params=_cparams(("arbitrary", "arbitrary"), 40 * MIB),
        name="mla_attention",
    )(qn, qp, kv_lat, kv_lat, kp_lat, kv_ctx, kv_ctx, kp_ctx)


def kernel(x, c, ctx, c_ctx, mod_w, mod_b, norm_mix, norm_mlp, mlp_w1, mlp_w2, ab_w_in, ab_w_out, na_rpb, lru_conv_w, lru_conv_b, lru_wa, lru_ba, lru_wx, lru_bx, lru_lambda, mla_w_dq, mla_w_dkv, mla_q_norm, mla_w_uq, mla_kv_norm, mla_w_ukv, mla_w_o, final_norm):
    xl = x.reshape(B * S, D)
    zc = ctx.reshape(B * C, D)

    groups = 16
    cond = jnp.concatenate([c, c_ctx[None, :], jnp.zeros((groups - B - 1, D), F32)], axis=0)
    mods = _mod_call(cond, mod_w, mod_b).reshape(2, groups, N_MOD, 1, D)

    def mod_lat(layer, k):
        return mods[layer, :B, k]

    def mod_ctx(layer, k):
        return mods[layer, B:B + 1, k]

    w_in = ab_w_in[0].astype(BF16)
    w_out = ab_w_out[0].astype(BF16)
    gain = norm_mix[0][None, :]
    p_lat = _proj_call(xl, gain, mod_lat(0, 0), mod_lat(0, 1), w_in, S, 1024, 1024)
    p_ctx = _proj_call(zc, gain, mod_ctx(0, 0), mod_ctx(0, 1), w_in, B * C, 1024, 1024)

    na_lat, na_ctx = _na_call(p_lat, p_ctx, _na_bias_table(na_rpb[0]))
    lru_lat, lru_ctx = _lru_call(p_lat, p_ctx, lru_conv_w[0], lru_conv_b[0],
                                 lru_wa[0].astype(BF16), lru_ba[0], lru_wx[0].astype(BF16), lru_bx[0],
                                 lru_lambda[0])

    xl = _out_call([na_lat, lru_lat], w_out, xl, mod_lat(0, 2), S, 512)
    zc = _out_call([na_ctx, lru_ctx], w_out, zc, mod_ctx(0, 2), B * C, 512)

    w1 = mlp_w1[0].astype(BF16)
    w2 = mlp_w2[0].astype(BF16)
    gain = norm_mlp[0][None, :]
    xl = _mlp_call(xl, gain, mod_lat(0, 3), mod_lat(0, 4), mod_lat(0, 5), w1, w2, S, 512, 1024)
    zc = _mlp_call(zc, gain, mod_ctx(0, 3), mod_ctx(0, 4), mod_ctx(0, 5), w1, w2, B * C, 512, 1024)

    gain = norm_mix[1][None, :]
    w_dkv = mla_w_dkv[0]
    w_pe = w_dkv[:, KV_LORA:]
    w_down_lat = jnp.concatenate([mla_w_dq[0], w_dkv[:, :KV_LORA], w_pe, _swap_cols(w_pe)], axis=1).astype(BF16)
    w_down_ctx = jnp.concatenate([w_dkv[:, :KV_LORA], w_pe, jnp.zeros_like(w_pe)], axis=1).astype(BF16)
    cs = _rope_tables()
    q_norm = mla_q_norm[0][None, :]
    kv_norm = mla_kv_norm[0][None, :]
    cq, ckv_lat, kp_lat = _mla_down_call(xl, gain, mod_lat(1, 0), mod_lat(1, 1), w_down_lat, q_norm, kv_norm,
                                         cs, S, 512, True, True)
    ckv_ctx, kp_ctx = _mla_down_call(zc, gain, mod_ctx(1, 0), mod_ctx(1, 1), w_down_ctx, q_norm, kv_norm,
                                     cs, B * C, 512, False, False)

    w_uq = mla_w_uq[0].reshape(Q_LORA, MLA_HEADS, NOPE + ROPE)
    w_uq_pe = w_uq[..., NOPE:]
    w_uq_ext = jnp.concatenate([w_uq, _swap_cols(w_uq_pe)], axis=-1).reshape(
        Q_LORA, MLA_HEADS * (NOPE + 2 * ROPE)).astype(BF16)
    qn, qp = _uq_call(cq, w_uq_ext, cs, 1024, 4)

    w_ukv = mla_w_ukv[0].astype(BF16)
    kv_lat = _mm_call(ckv_lat, w_ukv, 1024, 1024)
    kv_ctx = _mm_call(ckv_ctx, w_ukv, 1024, 1024)

    attn = _mla_attn_call(qn, qp, kv_lat, kp_lat, kv_ctx, kp_ctx)
    xl = _out_call([attn], mla_w_o[0].astype(BF16), xl, mod_lat(1, 2), S, 512)

    xl = _mlp_call(xl, norm_mlp[1][None, :], mod_lat(1, 3), mod_lat(1, 4), mod_lat(1, 5),
                   mlp_w1[1].astype(BF16), mlp_w2[1].astype(BF16), S, 512, 1024,
                   final_gain=final_norm[None, :])
    return xl.reshape(B, S, D)
```

```python
import functools

import numpy as np
import jax
import jax.numpy as jnp
from jax import lax
from jax.experimental import pallas as pl
from jax.experimental.pallas import tpu as pltpu

F32 = jnp.float32
BF16 = jnp.bfloat16

D = 2048
B = 8
S = 2048
C = 256
GRID_W = 64
ROWS = S // GRID_W
EPS = 1e-6
N_MOD = 6
D_FF = 4 * D
HD = 128
NA_HEADS = 8
NA_W = NA_HEADS * HD
NA_KR = 8
NA_KC = 16
LRU_W = D - NA_W
LRU_BLOCKS = 8
LRU_C = 8.0
AB_IN = 3 * NA_W + 2 * LRU_W
MLA_HEADS = 16
Q_LORA = 512
KV_LORA = 512
NOPE = 128
ROPE = 64
ROPE_BASE = 10000.0
NEG_INF = -1e30

MIB = 1 << 20
VMEM_CAP = 58 * MIB

LOG2E = float(np.log2(np.e))
ATT_TK = 256

NA_QR = 4
NA_KROWS = 12
NA_TQ = NA_QR * GRID_W
NA_TK = NA_KROWS * GRID_W
NA_BLOCKS = ROWS // NA_QR

LRU_TC = 32
LRU_T = C + S
LRU_CTX0 = 2
LRU_LAT0 = LRU_CTX0 + C + 2
LRU_UROWS = LRU_LAT0 + S + 4


def _cparams(sem, vmem_bytes):
    want = vmem_bytes + vmem_bytes // 4 + 8 * MIB
    return pltpu.CompilerParams(dimension_semantics=sem,
                                vmem_limit_bytes=int(min(max(want, 24 * MIB), VMEM_CAP)))


def _sigmoid(x):
    return 0.5 * jnp.tanh(0.5 * x) + 0.5


def _norm_mod(x, gain, shift, scale):
    ms = jnp.mean(x * x, axis=-1, keepdims=True)
    y = x * lax.rsqrt(ms + EPS) * gain
    return y * (1.0 + scale) + shift


def _rms(x, gain):
    ms = jnp.mean(x * x, axis=-1, keepdims=True)
    return x * lax.rsqrt(ms + EPS) * gain


def _mod_kernel(cond_ref, w_ref, b_ref, o_ref):
    c = cond_ref[...]
    s = c * _sigmoid(c)
    o_ref[...] = jnp.dot(s.astype(BF16), w_ref[...].astype(BF16),
                         preferred_element_type=F32) + b_ref[...]


def _mod_call(cond, mod_w, mod_b):
    depth, _, n = mod_w.shape
    g = cond.shape[0]
    tn = 1024
    return pl.pallas_call(
        _mod_kernel,
        grid=(depth, n // tn),
        in_specs=[pl.BlockSpec((g, D), lambda l, j: (0, 0)),
                  pl.BlockSpec((None, D, tn), lambda l, j: (l, 0, j)),
                  pl.BlockSpec((None, 1, tn), lambda l, j: (l, 0, j))],
        out_specs=pl.BlockSpec((None, g, tn), lambda l, j: (l, 0, j)),
        out_shape=jax.ShapeDtypeStruct((depth, g, n), F32),
        compiler_params=_cparams(("arbitrary", "arbitrary"), 2 * D * tn * 4 + 8 * MIB),
        name="adaln_mod",
    )(cond, mod_w, mod_b.reshape(depth, 1, n))


def _proj_kernel(x_ref, gain_ref, shift_ref, scale_ref, w_ref, o_ref, h_ref):
    @pl.when(pl.program_id(1) == 0)
    def _():
        h_ref[...] = _norm_mod(x_ref[...], gain_ref[...], shift_ref[...], scale_ref[...]).astype(BF16)

    o_ref[...] = jnp.dot(h_ref[...], w_ref[...], preferred_element_type=F32).astype(o_ref.dtype)


def _proj_call(x, gain, shift, scale, w, rows_per_group, tm, tn):
    m = x.shape[0]
    n = w.shape[1]
    tpg = rows_per_group // tm
    vmem = 2 * tm * D * 4 + tm * D * 2 + 2 * D * tn * 2 + 2 * tm * tn * 2 + tm * tn * 4 + 6 * MIB
    return pl.pallas_call(
        _proj_kernel,
        grid=(m // tm, n // tn),
        in_specs=[pl.BlockSpec((tm, D), lambda i, j: (i, 0)),
                  pl.BlockSpec((1, D), lambda i, j: (0, 0)),
                  pl.BlockSpec((None, 1, D), lambda i, j: (i // tpg, 0, 0)),
                  pl.BlockSpec((None, 1, D), lambda i, j: (i // tpg, 0, 0)),
                  pl.BlockSpec((D, tn), lambda i, j: (0, j))],
        out_specs=pl.BlockSpec((tm, tn), lambda i, j: (i, j)),
        out_shape=jax.ShapeDtypeStruct((m, n), BF16),
        scratch_shapes=[pltpu.VMEM((tm, D), BF16)],
        compiler_params=_cparams(("arbitrary", "arbitrary"), vmem),
        name="norm_mod_proj",
    )(x, gain, shift, scale, w)


def _mm_kernel(a_ref, w_ref, o_ref):
    o_ref[...] = jnp.dot(a_ref[...], w_ref[...], preferred_element_type=F32).astype(o_ref.dtype)


def _mm_call(a, w, tm, tn):
    m, k = a.shape
    n = w.shape[1]
    vmem = 2 * tm * k * 2 + 2 * k * tn * 2 + 2 * tm * tn * 2 + tm * tn * 4 + 6 * MIB
    return pl.pallas_call(
        _mm_kernel,
        grid=(m // tm, n // tn),
        in_specs=[pl.BlockSpec((tm, k), lambda i, j: (i, 0)),
                  pl.BlockSpec((k, tn), lambda i, j: (0, j))],
        out_specs=pl.BlockSpec((tm, tn), lambda i, j: (i, j)),
        out_shape=jax.ShapeDtypeStruct((m, n), BF16),
        compiler_params=_cparams(("arbitrary", "arbitrary"), vmem),
        name="matmul",
    )(a, w)


def _out_kernel(*refs, n_a):
    a_refs = refs[:n_a]
    w_ref, x_ref, gate_ref, o_ref = refs[n_a:]
    y = None
    k0 = 0
    for a_ref in a_refs:
        k = a_ref.shape[1]
        part = jnp.dot(a_ref[...], w_ref[k0:k0 + k, :], preferred_element_type=F32)
        y = part if y is None else y + part
        k0 += k
    o_ref[...] = x_ref[...] + gate_ref[...] * y


def _out_call(a_list, w, x, gate, rows_per_group, tm):
    m = x.shape[0]
    tpg = rows_per_group // tm
    ktot = w.shape[0]
    vmem = 2 * tm * ktot * 2 + 2 * ktot * D * 2 + 4 * tm * D * 4 + tm * D * 4 + 6 * MIB
    return pl.pallas_call(
        functools.partial(_out_kernel, n_a=len(a_list)),
        grid=(m // tm,),
        in_specs=[pl.BlockSpec((tm, a.shape[1]), lambda i: (i, 0)) for a in a_list]
        + [pl.BlockSpec((ktot, D), lambda i: (0, 0)),
           pl.BlockSpec((tm, D), lambda i: (i, 0)),
           pl.BlockSpec((None, 1, D), lambda i: (i // tpg, 0, 0))],
        out_specs=pl.BlockSpec((tm, D), lambda i: (i, 0)),
        out_shape=jax.ShapeDtypeStruct((m, D), F32),
        compiler_params=_cparams(("arbitrary",), vmem),
        name="out_proj_residual",
    )(*a_list, w, x, gate)


def _mlp_kernel(x_ref, gain_ref, shift_ref, scale_ref, gate_ref, w1_ref, w2_ref, fgain_ref,
                o_ref, h_ref, acc_ref, *, final_norm):
    f = pl.program_id(1)

    @pl.when(f == 0)
    def _():
        h_ref[...] = _norm_mod(x_ref[...], gain_ref[...], shift_ref[...], scale_ref[...]).astype(BF16)
        acc_ref[...] = jnp.zeros_like(acc_ref)

    a = jnp.dot(h_ref[...], w1_ref[...], preferred_element_type=F32)
    a = jnp.square(jnp.maximum(a, 0.0)).astype(BF16)
    acc_ref[...] += jnp.dot(a, w2_ref[...], preferred_element_type=F32)

    @pl.when(f == pl.num_programs(1) - 1)
    def _():
        y = x_ref[...] + gate_ref[...] * acc_ref[...]
        if final_norm:
            y = _rms(y, fgain_ref[...])
        o_ref[...] = y


def _mlp_call(x, gain, shift, scale, gate, w1, w2, rows_per_group, tm, tf, final_gain=None):
    m = x.shape[0]
    tpg = rows_per_group // tm
    final_norm = final_gain is not None
    fg = final_gain if final_norm else gain
    vmem = (4 * tm * D * 4 + tm * D * 2 + tm * D * 4 + 4 * D * tf * 2 + tm * tf * 6 + 6 * MIB)
    modspec = pl.BlockSpec((None, 1, D), lambda i, f: (i // tpg, 0, 0))
    vecspec = pl.BlockSpec((1, D), lambda i, f: (0, 0))
    return pl.pallas_call(
        functools.partial(_mlp_kernel, final_norm=final_norm),
        grid=(m // tm, D_FF // tf),
        in_specs=[pl.BlockSpec((tm, D), lambda i, f: (i, 0)),
                  vecspec, modspec, modspec, modspec,
                  pl.BlockSpec((D, tf), lambda i, f: (0, f)),
                  pl.BlockSpec((tf, D), lambda i, f: (f, 0)),
                  vecspec],
        out_specs=pl.BlockSpec((tm, D), lambda i, f: (i, 0)),
        out_shape=jax.ShapeDtypeStruct((m, D), F32),
        scratch_shapes=[pltpu.VMEM((tm, D), BF16), pltpu.VMEM((tm, D), F32)],
        compiler_params=_cparams(("arbitrary", "arbitrary"), vmem),
        name="mlp_relu2",
    )(x, gain, shift, scale, gate, w1, w2, fg)


def _na_block_start(blk):
    return min(max(NA_QR * blk - NA_KR // 2, 0), ROWS - NA_KROWS)


def _na_bias_table(rpb):
    heads = rpb.shape[0]
    span = GRID_W - NA_KC
    ext = jnp.concatenate([jnp.repeat(rpb[..., :1], span, axis=-1), rpb,
                           jnp.repeat(rpb[..., -1:], span, axis=-1)], axis=-1)
    toep = jnp.stack([ext[..., GRID_W - 1 - qc:2 * GRID_W - 1 - qc] for qc in range(GRID_W)], axis=2)
    col = np.arange(GRID_W)
    col_start = np.clip(col - NA_KC // 2, 0, GRID_W - NA_KC)
    col_ok = (col[None, :] >= col_start[:, None]) & (col[None, :] < col_start[:, None] + NA_KC)
    toep = jnp.where(col_ok[None, None], toep * LOG2E, NEG_INF)
    cfgs = []
    for blk in (0, 1, NA_BLOCKS - 1):
        start = _na_block_start(blk)
        r = NA_QR * blk + np.arange(NA_QR)
        kr = start + np.arange(NA_KROWS)
        row_start = np.clip(r - NA_KR // 2, 0, ROWS - NA_KR)
        row_ok = (kr[None, :] >= row_start[:, None]) & (kr[None, :] < row_start[:, None] + NA_KR)
        dr = np.clip(kr[None, :] - r[:, None] + (NA_KR - 1), 0, 2 * NA_KR - 2)
        blocks = jnp.where(row_ok[None, :, :, None, None], toep[:, dr], NEG_INF)
        cfgs.append(blocks.transpose(0, 1, 3, 2, 4).reshape(heads, NA_TQ, NA_TK))
    return jnp.stack(cfgs, axis=1)


def _qk(q, k):
    return lax.dot_general(q, k, (((1,), (1,)), ((), ())), preferred_element_type=F32)


def _fill_ones_column(v_ref):
    rows, width = v_ref.shape
    lane = lax.broadcasted_iota(jnp.int32, (rows, width - HD), 1)
    v_ref[:, HD:] = jnp.where(lane == 0, 1.0, 0.0).astype(v_ref.dtype)


def _attend(q, n_tiles, k_tile, v_tile, bias_tile, s_ref):
    half = ATT_TK // 2
    m_acc = None
    for t in range(n_tiles):
        s = _qk(q, k_tile(t))
        b = bias_tile(t)
        if b is not None:
            s = s + b
        s_ref[:, t * ATT_TK:(t + 1) * ATT_TK] = s
        mt = jnp.maximum(s[:, :half], s[:, half:])
        m_acc = mt if m_acc is None else jnp.maximum(m_acc, mt)
    m = jnp.max(m_acc, axis=-1, keepdims=True)
    out = None
    for t in range(n_tiles):
        p = jnp.exp2(s_ref[:, t * ATT_TK:(t + 1) * ATT_TK] - m)
        ot = jnp.dot(p.astype(BF16), v_tile(t), preferred_element_type=F32)
        out = ot if out is None else out + ot
    return out[:, :HD] / out[:, HD:HD + 1]


def _na_kernel(q_ref, k_ref, v_ref, qc_ref, kc_ref, vc_ref, bias_ref, o_ref, oc_ref, vx_ref, s_ref):
    n_win = NA_TK // ATT_TK
    vx_ref[0:S, 0:HD] = v_ref[...]
    vx_ref[S:, 0:HD] = vc_ref[...]
    _fill_ones_column(vx_ref)
    for blk in range(NA_BLOCKS):
        cfg = 0 if blk == 0 else (2 if blk == NA_BLOCKS - 1 else 1)
        k0 = _na_block_start(blk) * GRID_W
        q = q_ref[blk * NA_TQ:(blk + 1) * NA_TQ, :]

        def k_tile(t, k0=k0):
            return k_ref[k0 + t * ATT_TK:k0 + (t + 1) * ATT_TK, :] if t < n_win else kc_ref[...]

        def v_tile(t, k0=k0):
            return vx_ref[k0 + t * ATT_TK:k0 + (t + 1) * ATT_TK, :] if t < n_win else vx_ref[S:, :]

        def bias_tile(t, cfg=cfg):
            return bias_ref[cfg, :, t * ATT_TK:(t + 1) * ATT_TK] if t < n_win else None

        o = _attend(q, n_win + 1, k_tile, v_tile, bias_tile, s_ref.at[blk % 2])
        o_ref[blk * NA_TQ:(blk + 1) * NA_TQ, :] = o.astype(o_ref.dtype)
    oc = _attend(qc_ref[...], 1, lambda t: kc_ref[...], lambda t: vx_ref[S:, :], lambda t: None, s_ref.at[0])
    oc_ref[...] = oc.astype(oc_ref.dtype)


def _na_call(p_lat, p_ctx, bias):
    lat = lambda off: pl.BlockSpec((S, HD), lambda h, b: (b, off + h))
    ctx = lambda off: pl.BlockSpec((C, HD), lambda h, b: (b, off + h))
    vmem = 2 * NA_TQ * (NA_TK + C) * 4 + 2 * 3 * NA_TQ * NA_TK * 4 + 2 * 4 * S * HD * 2 + 4 * MIB
    return pl.pallas_call(
        _na_kernel,
        grid=(NA_HEADS, B),
        in_specs=[lat(0), lat(NA_HEADS), lat(2 * NA_HEADS),
                  ctx(0), ctx(NA_HEADS), ctx(2 * NA_HEADS),
                  pl.BlockSpec((None, 3, NA_TQ, NA_TK), lambda h, b: (h, 0, 0, 0))],
        out_specs=[pl.BlockSpec((S, HD), lambda h, b: (b, h)),
                   pl.BlockSpec((C, HD), lambda h, b: (b, h))],
        out_shape=[jax.ShapeDtypeStruct((B * S, NA_W), BF16),
                   jax.ShapeDtypeStruct((B * C, NA_W), BF16)],
        scratch_shapes=[pltpu.VMEM((S + C, 2 * HD), BF16), pltpu.VMEM((2, NA_TQ, NA_TK + C), F32)],
        compiler_params=_cparams(("arbitrary", "arbitrary"), vmem),
        name="neighbourhood_attention",
    )(p_lat, p_lat, p_lat, p_ctx, p_ctx, p_ctx, bias)


def _gelu_tanh(x):
    return 0.5 * x * (1.0 + jnp.tanh(np.sqrt(2.0 / np.pi) * (x + 0.044715 * (x * x * x))))


def _lru_kernel(ul_ref, uc_ref, gl_ref, gc_ref, cw_ref, cb_ref, wa_ref, ba_ref, wx_ref, bx_ref,
                lam_ref, ol_ref, oc_ref, ut_ref, yt_ref):
    tc = LRU_TC
    chunk = 256
    ut_ref[0:LRU_CTX0] = jnp.zeros((LRU_CTX0, B, HD), F32)
    ut_ref[LRU_CTX0:LRU_CTX0 + C] = jnp.swapaxes(uc_ref[...].astype(F32), 0, 1)
    ut_ref[LRU_CTX0 + C:LRU_LAT0] = jnp.zeros((LRU_LAT0 - LRU_CTX0 - C, B, HD), F32)
    for c in range(S // chunk):
        ut_ref[LRU_LAT0 + c * chunk:LRU_LAT0 + (c + 1) * chunk] = jnp.swapaxes(
            ul_ref[:, c * chunk:(c + 1) * chunk, :].astype(F32), 0, 1)
    ut_ref[LRU_LAT0 + S:LRU_UROWS] = jnp.zeros((LRU_UROWS - LRU_LAT0 - S, B, HD), F32)

    cw = cw_ref[...]
    cb = cb_ref[...]
    neg = -lam_ref[...]
    softplus = jnp.maximum(neg, 0.0) + jnp.log1p(jnp.exp(-jnp.abs(neg)))
    half_c = (-0.5 * LRU_C) * softplus
    half_ba = 0.5 * ba_ref[...]
    half_bx = 0.5 * bx_ref[...]

    def chunk_step(base_u, base_y, d, h, reverse, accumulate):
        xc = cb
        for tap in range(4):
            xc = xc + ut_ref[pl.ds(base_u + (tap - 2), tc)] * cw[tap:tap + 1, :]
        x2 = xc.reshape(tc * B, HD)
        xb = x2.astype(BF16)
        tanh_r = jnp.tanh(jnp.dot(xb, wa_ref[d], preferred_element_type=F32) + half_ba[d:d + 1, :])
        tanh_i = jnp.tanh(jnp.dot(xb, wx_ref[d], preferred_element_type=F32) + half_bx[d:d + 1, :])
        log_a = half_c[d:d + 1, :] * tanh_r + half_c[d:d + 1, :]
        a = jnp.exp(log_a)
        quarter = jnp.tanh(log_a) * (a * a * -0.25 - 0.25)
        v = jnp.sqrt(quarter) * ((tanh_i + 1.0) * x2)
        a3 = a.reshape(tc, B, HD)
        v3 = v.reshape(tc, B, HD)
        hs = [None] * tc
        for i in (range(tc - 1, -1, -1) if reverse else range(tc)):
            h = a3[i] * h + v3[i]
            hs[i] = h
        y = jnp.stack(hs, axis=0)
        if accumulate:
            yt_ref[pl.ds(base_y, tc)] += y
        else:
            yt_ref[pl.ds(base_y, tc)] = y
        return h

    def pair_loop(n, u0, y0, lo, hi, accumulate, carry):
        def body(j, hc):
            jr = n - 1 - j
            hf = chunk_step(u0 + j * tc, y0 + j * tc, 0, hc[0], False, accumulate)
            hr = chunk_step(u0 + jr * tc, y0 + jr * tc, 1, hc[1], True, accumulate)
            return hf, hr
        return lax.fori_loop(lo, hi, body, carry)

    h0 = jnp.zeros((B, HD), F32)
    n_ctx = C // tc
    n_lat = S // tc
    carry = pair_loop(n_ctx, LRU_CTX0, 0, 0, n_ctx // 2, False, (h0, h0))
    carry = pair_loop(n_ctx, LRU_CTX0, 0, n_ctx // 2, n_ctx, True, carry)
    carry = pair_loop(n_lat, LRU_LAT0, C, 0, n_lat // 2, False, carry)
    pair_loop(n_lat, LRU_LAT0, C, n_lat // 2, n_lat, True, carry)

    yc = jnp.swapaxes(yt_ref[0:C], 0, 1)
    oc_ref[...] = (yc * _gelu_tanh(gc_ref[...].astype(F32))).astype(oc_ref.dtype)
    for c in range(S // chunk):
        yl = jnp.swapaxes(yt_ref[C + c * chunk:C + (c + 1) * chunk], 0, 1)
        g = gl_ref[:, c * chunk:(c + 1) * chunk, :].astype(F32)
        ol_ref[:, c * chunk:(c + 1) * chunk, :] = (yl * _gelu_tanh(g)).astype(ol_ref.dtype)


def _lru_call(p_lat, p_ctx, conv_w, conv_b, wa, ba, wx, bx, lam):
    u_off = 3 * NA_W // HD
    g_off = u_off + LRU_W // HD
    p_lat3 = p_lat.reshape(B, S, AB_IN)
    p_ctx3 = p_ctx.reshape(B, C, AB_IN)
    lat = lambda off: pl.BlockSpec((B, S, HD), lambda n: (0, 0, off + n))
    ctx = lambda off: pl.BlockSpec((B, C, HD), lambda n: (0, 0, off + n))
    vec = lambda rows: pl.BlockSpec((rows, HD), lambda n: (0, n))
    wspec = pl.BlockSpec((2, None, HD, HD), lambda n: (0, n, 0, 0))
    vmem = (LRU_UROWS + LRU_T) * B * HD * 4 + 2 * 3 * (B * S * HD * 2) + 2 * 3 * (B * C * HD * 2) + 8 * MIB
    o_lat, o_ctx = pl.pallas_call(
        _lru_kernel,
        grid=(LRU_BLOCKS,),
        in_specs=[lat(u_off), ctx(u_off), lat(g_off), ctx(g_off),
                  vec(4), vec(1), wspec, vec(2), wspec, vec(2), vec(2)],
        out_specs=[pl.BlockSpec((B, S, HD), lambda n: (0, 0, n)),
                   pl.BlockSpec((B, C, HD), lambda n: (0, 0, n))],
        out_shape=[jax.ShapeDtypeStruct((B, S, LRU_W), BF16),
                   jax.ShapeDtypeStruct((B, C, LRU_W), BF16)],
        scratch_shapes=[pltpu.VMEM((LRU_UROWS, B, HD), F32), pltpu.VMEM((LRU_T, B, HD), F32)],
        compiler_params=_cparams(("arbitrary",), vmem),
        name="bidirectional_rglru",
    )(p_lat3, p_ctx3, p_lat3, p_ctx3, conv_w, conv_b.reshape(1, LRU_W), wa, ba, wx, bx, lam)
    return o_lat.reshape(B * S, LRU_W), o_ctx.reshape(B * C, LRU_W)


def _rope_tables():
    pos = np.arange(S)
    row = (pos // GRID_W).astype(np.float32)
    col = (pos % GRID_W).astype(np.float32)
    axis_dim = ROPE // 2
    inv = jnp.power(ROPE_BASE, -jnp.arange(0, axis_dim, 2, dtype=F32) / axis_dim)
    ang_r = jnp.asarray(row)[:, None] * inv
    ang_c = jnp.asarray(col)[:, None] * inv
    cos = jnp.concatenate([jnp.cos(ang_r), jnp.cos(ang_r), jnp.cos(ang_c), jnp.cos(ang_c)], axis=-1)
    sin = jnp.concatenate([-jnp.sin(ang_r), jnp.sin(ang_r), -jnp.sin(ang_c), jnp.sin(ang_c)], axis=-1)
    return jnp.concatenate([cos, sin], axis=-1)


def _swap_cols(w):
    q = ROPE // 4
    return jnp.concatenate([w[..., q:2 * q], w[..., :q], w[..., 3 * q:], w[..., 2 * q:3 * q]], axis=-1)


def _rope_from_pair(pair, cs):
    r = pair * cs
    r = r + pltpu.roll(r, ROPE, axis=1)
    lane = lax.broadcasted_iota(jnp.int32, r.shape, 1)
    return jnp.where(lane < ROPE, r, 0.0)


def _mla_down_kernel(x_ref, gain_ref, shift_ref, scale_ref, w_ref, qn_ref, kvn_ref, cs_ref,
                     *out_refs, has_q, use_rope):
    h = _norm_mod(x_ref[...], gain_ref[...], shift_ref[...], scale_ref[...]).astype(BF16)
    p = jnp.dot(h, w_ref[...], preferred_element_type=F32)
    off = 0
    outs = list(out_refs)
    if has_q:
        oq_ref = outs.pop(0)
        oq_ref[...] = _rms(p[:, :Q_LORA], qn_ref[...]).astype(oq_ref.dtype)
        off = Q_LORA
    okv_ref, okp_ref = outs
    okv_ref[...] = _rms(p[:, off:off + KV_LORA], kvn_ref[...]).astype(okv_ref.dtype)
    pair = p[:, off + KV_LORA:off + KV_LORA + 2 * ROPE]
    if use_rope:
        kp = _rope_from_pair(pair, cs_ref[...])
    else:
        lane = lax.broadcasted_iota(jnp.int32, pair.shape, 1)
        kp = jnp.where(lane < ROPE, pair, 0.0)
    okp_ref[...] = kp.astype(okp_ref.dtype)


def _mla_down_call(x, gain, shift, scale, w, q_norm, kv_norm, cs, rows_per_group, tm, has_q, use_rope):
    m = x.shape[0]
    n = w.shape[1]
    tpg = rows_per_group // tm
    tps = S // tm if use_rope else 1
    out_shape = [jax.ShapeDtypeStruct((m, KV_LORA), BF16), jax.ShapeDtypeStruct((m, 2 * ROPE), BF16)]
    out_specs = [pl.BlockSpec((tm, KV_LORA), lambda i: (i, 0)), pl.BlockSpec((tm, 2 * ROPE), lambda i: (i, 0))]
    if has_q:
        out_shape.insert(0, jax.ShapeDtypeStruct((m, Q_LORA), BF16))
        out_specs.insert(0, pl.BlockSpec((tm, Q_LORA), lambda i: (i, 0)))
    vmem = 2 * tm * D * 4 + tm * D * 2 + 2 * D * n * 2 + tm * n * 4 + 4 * tm * n * 2 + 6 * MIB
    return pl.pallas_call(
        functools.partial(_mla_down_kernel, has_q=has_q, use_rope=use_rope),
        grid=(m // tm,),
        in_specs=[pl.BlockSpec((tm, D), lambda i: (i, 0)),
                  pl.BlockSpec((1, D), lambda i: (0, 0)),
                  pl.BlockSpec((None, 1, D), lambda i: (i // tpg, 0, 0)),
                  pl.BlockSpec((None, 1, D), lambda i: (i // tpg, 0, 0)),
                  pl.BlockSpec((D, n), lambda i: (0, 0)),
                  pl.BlockSpec((1, Q_LORA), lambda i: (0, 0)),
                  pl.BlockSpec((1, KV_LORA), lambda i: (0, 0)),
                  pl.BlockSpec((tm, 2 * ROPE), lambda i: (i % tps, 0))],
        out_specs=out_specs,
        out_shape=out_shape,
        compiler_params=_cparams(("arbitrary",), vmem),
        name="mla_down_proj",
    )(x, gain, shift, scale, w, q_norm, kv_norm, cs)


def _uq_kernel(a_ref, w_ref, cs_ref, qn_ref, qp_ref, *, heads):
    p = jnp.dot(a_ref[...], w_ref[...], preferred_element_type=F32) * ((NOPE + ROPE) ** -0.5 * LOG2E)
    cs = cs_ref[...]
    width = NOPE + 2 * ROPE
    for hh in range(heads):
        qn_ref[:, hh * NOPE:(hh + 1) * NOPE] = p[:, hh * width:hh * width + NOPE].astype(qn_ref.dtype)
        pair = p[:, hh * width + NOPE:(hh + 1) * width]
        qp_ref[:, hh * 2 * ROPE:(hh + 1) * 2 * ROPE] = _rope_from_pair(pair, cs).astype(qp_ref.dtype)


def _uq_call(a, w, cs, tm, heads_per_step):
    m, k = a.shape
    width = NOPE + 2 * ROPE
    tn = heads_per_step * width
    tps = S // tm
    vmem = 2 * tm * k * 2 + 2 * k * tn * 2 + tm * tn * 4 + 4 * tm * heads_per_step * HD * 2 + 8 * MIB
    return pl.pallas_call(
        functools.partial(_uq_kernel, heads=heads_per_step),
        grid=(m // tm, MLA_HEADS // heads_per_step),
        in_specs=[pl.BlockSpec((tm, k), lambda i, j: (i, 0)),
                  pl.BlockSpec((k, tn), lambda i, j: (0, j)),
                  pl.BlockSpec((tm, 2 * ROPE), lambda i, j: (i % tps, 0))],
        out_specs=[pl.BlockSpec((tm, heads_per_step * NOPE), lambda i, j: (i, j)),
                   pl.BlockSpec((tm, heads_per_step * 2 * ROPE), lambda i, j: (i, j))],
        out_shape=[jax.ShapeDtypeStruct((m, MLA_HEADS * NOPE), BF16),
                   jax.ShapeDtypeStruct((m, MLA_HEADS * 2 * ROPE), BF16)],
        compiler_params=_cparams(("arbitrary", "arbitrary"), vmem),
        name="mla_q_up_rope",
    )(a, w, cs)


MLA_TQ = 256


def _mla_attn_kernel(qn_ref, qp_ref, knl_ref, vl_ref, kpl_ref, knc_ref, vc_ref, kpc_ref, o_ref,
                     k_ref, v_ref, s_ref):
    k_ref[0:S, 0:NOPE] = knl_ref[...]
    k_ref[0:S, NOPE:] = kpl_ref[...]
    k_ref[S:, 0:NOPE] = knc_ref[...]
    k_ref[S:, NOPE:] = kpc_ref[...]
    v_ref[0:S, 0:HD] = vl_ref[...]
    v_ref[S:, 0:HD] = vc_ref[...]
    _fill_ones_column(v_ref)
    for qi in range(S // MLA_TQ):
        rows = slice(qi * MLA_TQ, (qi + 1) * MLA_TQ)
        q = jnp.concatenate([qn_ref[rows, :], qp_ref[rows, :]], axis=-1)
        o = _attend(q, (S + C) // ATT_TK,
                    lambda t: k_ref[t * ATT_TK:(t + 1) * ATT_TK, :],
                    lambda t: v_ref[t * ATT_TK:(t + 1) * ATT_TK, :],
                    lambda t: None, s_ref.at[qi % 2])
        o_ref[rows, :] = o.astype(o_ref.dtype)


def _mla_attn_call(qn, qp, kv_lat, kp_lat, kv_ctx, kp_ctx):
    vmem = 2 * MLA_TQ * (S + C) * 4 + (S + C) * 3 * HD * 2 + 2 * 6 * S * HD * 2 + 4 * MIB
    return pl.pallas_call(
        _mla_attn_kernel,
        grid=(B, MLA_HEADS),
        in_specs=[pl.BlockSpec((S, HD), lambda b, h: (b, h)),
                  pl.BlockSpec((S, HD), lambda b, h: (b, h)),
                  pl.BlockSpec((S, HD), lambda b, h: (b, 2 * h)),
                  pl.BlockSpec((S, HD), lambda b, h: (b, 2 * h + 1)),
                  pl.BlockSpec((S, HD), lambda b, h: (b, 0)),
                  pl.BlockSpec((C, HD), lambda b, h: (b, 2 * h)),
                  pl.BlockSpec((C, HD), lambda b, h: (b, 2 * h + 1)),
                  pl.BlockSpec((C, HD), lambda b, h: (b, 0))],
        out_specs=pl.BlockSpec((S, HD), lambda b, h: (b, h)),
        out_shape=jax.ShapeDtypeStruct((B * S, MLA_HEADS * HD), BF16),
        scratch_shapes=[pltpu.VMEM((S + C, 2 * HD), BF16), pltpu.VMEM((S + C, 2 * HD), BF16),
                        pltpu.VMEM((2, MLA_TQ, S + C), F32)],
        compiler_params=_cparams(("arbitrary", "arbitrary"), vmem),
        name="mla_attention",
    )(qn, qp, kv_lat, kv_lat, kp_lat, kv_ctx, kv_ctx, kp_ctx)


def kernel(x, c, ctx, c_ctx, mod_w, mod_b, norm_mix, norm_mlp, mlp_w1, mlp_w2, ab_w_in, ab_w_out, na_rpb, lru_conv_w, lru_conv_b, lru_wa, lru_ba, lru_wx, lru_bx, lru_lambda, mla_w_dq, mla_w_dkv, mla_q_norm, mla_w_uq, mla_kv_norm, mla_w_ukv, mla_w_o, final_norm):
    xl = x.reshape(B * S, D)
    zc = ctx.reshape(B * C, D)

    groups = 16
    cond = jnp.concatenate([c, c_ctx[None, :], jnp.zeros((groups - B - 1, D), F32)], axis=0)
    mods = _mod_call(cond, mod_w, mod_b).reshape(2, groups, N_MOD, 1, D)

    def mod_lat(layer, k):
        return mods[layer, :B, k]

    def mod_ctx(layer, k):
        return mods[layer, B:B + 1, k]

    col_scale = jnp.concatenate([jnp.full((NA_W,), HD ** -0.5 * LOG2E, F32), jnp.ones((AB_IN - NA_W,), F32)])
    w_in = (ab_w_in[0] * col_scale).astype(BF16)
    w_out = ab_w_out[0].astype(BF16)
    gain = norm_mix[0][None, :]
    p_lat = _proj_call(xl, gain, mod_lat(0, 0), mod_lat(0, 1), w_in, S, 1024, 1024)
    p_ctx = _proj_call(zc, gain, mod_ctx(0, 0), mod_ctx(0, 1), w_in, B * C, 1024, 1024)

    na_lat, na_ctx = _na_call(p_lat, p_ctx, _na_bias_table(na_rpb[0]))
    lru_lat, lru_ctx = _lru_call(p_lat, p_ctx, lru_conv_w[0], lru_conv_b[0],
                                 (0.5 * lru_wa[0]).astype(BF16), lru_ba[0], (0.5 * lru_wx[0]).astype(BF16),
                                 lru_bx[0], lru_lambda[0])

    xl = _out_call([na_lat, lru_lat], w_out, xl, mod_lat(0, 2), S, 512)
    zc = _out_call([na_ctx, lru_ctx], w_out, zc, mod_ctx(0, 2), B * C, 512)

    w1 = mlp_w1[0].astype(BF16)
    w2 = mlp_w2[0].astype(BF16)
    gain = norm_mlp[0][None, :]
    xl = _mlp_call(xl, gain, mod_lat(0, 3), mod_lat(0, 4), mod_lat(0, 5), w1, w2, S, 512, 1024)
    zc = _mlp_call(zc, gain, mod_ctx(0, 3), mod_ctx(0, 4), mod_ctx(0, 5), w1, w2, B * C, 512, 1024)

    gain = norm_mix[1][None, :]
    w_dkv = mla_w_dkv[0]
    w_pe = w_dkv[:, KV_LORA:]
    w_down_lat = jnp.concatenate([mla_w_dq[0], w_dkv[:, :KV_LORA], w_pe, _swap_cols(w_pe)], axis=1).astype(BF16)
    w_down_ctx = jnp.concatenate([w_dkv[:, :KV_LORA], w_pe, jnp.zeros_like(w_pe)], axis=1).astype(BF16)
    cs = _rope_tables()
    q_norm = mla_q_norm[0][None, :]
    kv_norm = mla_kv_norm[0][None, :]
    cq, ckv_lat, kp_lat = _mla_down_call(xl, gain, mod_lat(1, 0), mod_lat(1, 1), w_down_lat, q_norm, kv_norm,
                                         cs, S, 512, True, True)
    ckv_ctx, kp_ctx = _mla_down_call(zc, gain, mod_ctx(1, 0), mod_ctx(1, 1), w_down_ctx, q_norm, kv_norm,
                                     cs, B * C, 512, False, False)

    w_uq = mla_w_uq[0].reshape(Q_LORA, MLA_HEADS, NOPE + ROPE)
    w_uq_pe = w_uq[..., NOPE:]
    w_uq_ext = jnp.concatenate([w_uq, _swap_cols(w_uq_pe)], axis=-1).reshape(
        Q_LORA, MLA_HEADS * (NOPE + 2 * ROPE)).astype(BF16)
    qn, qp = _uq_call(cq, w_uq_ext, cs, 1024, 4)

    w_ukv = mla_w_ukv[0].astype(BF16)
    kv_lat = _mm_call(ckv_lat, w_ukv, 1024, 1024)
    kv_ctx = _mm_call(ckv_ctx, w_ukv, 1024, 1024)

    attn = _mla_attn_call(qn, qp, kv_lat, kp_lat, kv_ctx, kp_ctx)
    xl = _out_call([attn], mla_w_o[0].astype(BF16), xl, mod_lat(1, 2), S, 512)

    xl = _mlp_call(xl, norm_mlp[1][None, :], mod_lat(1, 3), mod_lat(1, 4), mod_lat(1, 5),
                   mlp_w1[1].astype(BF16), mlp_w2[1].astype(BF16), S, 512, 1024,
                   final_gain=final_norm[None, :])
    return xl.reshape(B, S, D)
```
